```python
import math
import jax
import jax.numpy as jnp
from jax import lax
import numpy as np

D_MODEL = 1024
BATCH = 2
SEQ = 16384
DEPTH = 4

N_MIXERS = 4
F32 = jnp.float32
ROPE_THETA = 10000.0
DEEPNORM_ALPHA = (2.0 * DEPTH) ** 0.25
DEEPNORM_BETA = (8.0 * DEPTH) ** -0.25
LN_EPS = 1e-5
NEG_INF = -1e30
D_FF = ((8 * D_MODEL + 3 * 256 - 1) // (3 * 256)) * 256

RWKV_HEAD_DIM = 64
RWKV_HEADS = D_MODEL // RWKV_HEAD_DIM
RWKV_LORA_DECAY = 64
RWKV_LORA_ICLR = 64
RWKV_LORA_GATE = 128
RWKV_GN_EPS = 64e-5
GDN_HEAD_DIM = 128
GDN_HEADS = D_MODEL // GDN_HEAD_DIM
GDN_CONV = 4
GDN_CHUNK = 64
GDN_QKV = 3 * GDN_HEADS * GDN_HEAD_DIM
GDN_PROJ = 4 * GDN_HEADS * GDN_HEAD_DIM + 2 * GDN_HEADS
DIFF_HEADS = 8
DIFF_HEAD_DIM = D_MODEL // (2 * DIFF_HEADS)
ATTN_Q_BLOCK = 128
SWA_HEAD_DIM = 64
SWA_Q_HEADS = D_MODEL // SWA_HEAD_DIM
SWA_KV_HEADS = SWA_Q_HEADS // 8
SWA_WINDOW = 128
SWA_QKV = (SWA_Q_HEADS + 2 * SWA_KV_HEADS) * SWA_HEAD_DIM

kernel_name = 'hybrid_interleaved_rwkv7_gdn_diffattn_swa'


def _layer_norm(x, g, b):
    xf = x.astype(F32)
    mu = jnp.mean(xf, -1, keepdims=True)
    var = jnp.mean(jnp.square(xf - mu), -1, keepdims=True)
    return ((xf - mu) * lax.rsqrt(var + LN_EPS) * g + b).astype(x.dtype)


def _rms_norm(x, g, eps):
    xf = x.astype(F32)
    return xf * lax.rsqrt(jnp.mean(xf * xf, -1, keepdims=True) + eps) * g


def _l2norm(x, eps=1e-6):
    xf = x.astype(F32)
    return (xf * lax.rsqrt(jnp.sum(xf * xf, -1, keepdims=True) + eps)).astype(x.dtype)


def _rope(x, pos):
    d = x.shape[-1]
    inv = ROPE_THETA ** (-jnp.arange(0, d, 2, dtype=F32) / d)
    ang = pos.astype(F32)[:, None] * inv[None, :]
    cos = jnp.cos(ang)[None, :, None, :]
    sin = jnp.sin(ang)[None, :, None, :]
    x1, x2 = jnp.split(x.astype(F32), 2, axis=-1)
    return jnp.concatenate([x1 * cos - x2 * sin, x2 * cos + x1 * sin], -1).astype(x.dtype)


def _token_shift(x):
    return jnp.pad(x, ((0, 0), (1, 0), (0, 0)))[:, :-1]


def _causal_depthwise_conv(x, w):
    K, C = w.shape
    return lax.conv_general_dilated(x, w[:, None, :].astype(x.dtype), window_strides=(1,),
                                    padding=[(K - 1, 0)], dimension_numbers=('NWC', 'WIO', 'NWC'),
                                    feature_group_count=C)


def _rwkv7_mixer(u, mu, w_rkv, w0, w1, w2, a0, a1, a2, g1, g2, k_k, k_a, r_k, gn_g, gn_b, w_out):
    B, T, D = u.shape
    H, N = RWKV_HEADS, RWKV_HEAD_DIM
    xx = _token_shift(u) - u
    lerp = lambda i: u + xx * mu[i]
    r = lerp(0) @ w_rkv[0]
    k = lerp(1) @ w_rkv[1]
    v = lerp(2) @ w_rkv[2]
    w_log = -jax.nn.softplus(-(w0 + jnp.tanh(lerp(3) @ w1) @ w2)) - 0.5
    a = jax.nn.sigmoid(a0 + (lerp(4) @ a1) @ a2)
    g = jax.nn.sigmoid(lerp(5) @ g1) @ g2
    heads = lambda t: t.reshape(B, T, H, N)
    kk = _l2norm(heads(k * k_k))
    k = k * (1 + (a - 1) * k_a)
    r, k, v, a = heads(r), heads(k), heads(v), heads(a)
    decay = jnp.exp(-jnp.exp(heads(w_log).astype(F32)))
    tm = lambda t: jnp.moveaxis(t.astype(F32), 1, 0)

    def step(S, inp):
        r_t, k_t, v_t, w_t, a_t, b_t = inp
        sa = jnp.einsum('bhvk,bhk->bhv', S, a_t)
        S = S * w_t[:, :, None, :] + sa[..., None] * b_t[:, :, None, :] + v_t[..., None] * k_t[:, :, None, :]
        return S, jnp.einsum('bhvk,bhk->bhv', S, r_t)

    S0 = jnp.zeros((B, H, N, N), F32)
    _, y = lax.scan(step, S0, (tm(r), tm(k), tm(v), jnp.moveaxis(decay, 1, 0), tm(-kk), tm(kk * a)))
    y = jnp.moveaxis(y, 0, 1)
    mean = jnp.mean(y, -1, keepdims=True)
    var = jnp.mean(jnp.square(y - mean), -1, keepdims=True)
    y = ((y - mean) * lax.rsqrt(var + RWKV_GN_EPS)).reshape(B, T, D) * gn_g + gn_b
    bonus = jnp.sum((r * k * r_k).astype(F32), -1, keepdims=True) * v.astype(F32)
    y = (y + bonus.reshape(B, T, D)).astype(u.dtype)
    return (y * g) @ w_out


def _chunk_gated_delta_rule(q, k, v, g, beta):
    B, T, H, dk = q.shape
    dv = v.shape[-1]
    C = GDN_CHUNK
    n = T // C
    chunks = lambda t: jnp.moveaxis(t.astype(F32).reshape(B, n, C, H, -1), 3, 1)
    q, k, v = chunks(q), chunks(k), chunks(v)
    beta = chunks(beta[..., None])
    gc = jnp.cumsum(chunks(g[..., None])[..., 0], axis=-1)
    k_beta, v_beta = k * beta, v * beta
    idx = jnp.arange(C)
    causal = idx[:, None] >= idx[None, :]
    strict = idx[:, None] > idx[None, :]
    decay = jnp.exp(jnp.where(causal, gc[..., :, None] - gc[..., None, :], -jnp.inf))
    L = jnp.where(strict, jnp.einsum('bhnid,bhnjd->bhnij', k_beta, k) * decay, 0.0)
    eye = jnp.eye(C, dtype=F32)
    tinv = lax.linalg.triangular_solve(eye + L, jnp.broadcast_to(eye, L.shape), left_side=True, lower=True)
    u_c = tinv @ v_beta
    w_c = tinv @ (k_beta * jnp.exp(gc)[..., None])
    a_intra = jnp.where(causal, jnp.einsum('bhnid,bhnjd->bhnij', q, k) * decay, 0.0)
    q_dec = q * jnp.exp(gc)[..., None]
    k_dec = k * jnp.exp(gc[..., -1:] - gc)[..., None]
    g_last = jnp.exp(gc[..., -1])

    def step(S, inp):
        q_i, k_i, u_i, w_i, a_i, gl = inp
        v_new = u_i - w_i @ S
        o = q_i @ S + a_i @ v_new
        S = S * gl[..., None, None] + jnp.einsum('bhck,bhcv->bhkv', k_i, v_new)
        return S, o

    xs = tuple(jnp.moveaxis(t, 2, 0) for t in (q_dec, k_dec, u_c, w_c, a_intra, g_last))
    _, o = lax.scan(step, jnp.zeros((B, H, dk, dv), F32), xs)
    return jnp.transpose(o, (1, 0, 3, 2, 4)).reshape(B, T, H, dv)


def _gated_deltanet_mixer(u, w_in, conv_w, a_log, dt_bias, norm_g, w_out):
    B, T, _ = u.shape
    H, d = GDN_HEADS, GDN_HEAD_DIM
    hd = H * d
    proj = u @ w_in
    qkv = jax.nn.silu(_causal_depthwise_conv(proj[..., :3 * hd], conv_w))
    z = proj[..., 3 * hd:4 * hd].reshape(B, T, H, d)
    b_in = proj[..., 4 * hd:4 * hd + H]
    a_in = proj[..., 4 * hd + H:]
    q = _l2norm(qkv[..., :hd].reshape(B, T, H, d)) * d ** -0.5
    k = _l2norm(qkv[..., hd:2 * hd].reshape(B, T, H, d))
    v = qkv[..., 2 * hd:].reshape(B, T, H, d)
    beta = jax.nn.sigmoid(b_in.astype(F32))
    g = -jnp.exp(a_log.astype(F32)) * jax.nn.softplus(a_in.astype(F32) + dt_bias)
    o = _chunk_gated_delta_rule(q, k, v, g, beta)
    o = _rms_norm(o, norm_g, 1e-6) * jax.nn.silu(z.astype(F32))
    return o.reshape(B, T, hd).astype(u.dtype) @ w_out


def _diff_attention_mixer(u, pos, w_in, lam_p, subln_g, w_out, lam_init):
    B, T, D = u.shape
    H, d, Qb = DIFF_HEADS, DIFF_HEAD_DIM, ATTN_Q_BLOCK
    proj = u @ w_in
    q = _rope(proj[..., :D].reshape(B, T, 2 * H, d), pos) * d ** -0.5
    k = _rope(proj[..., D:2 * D].reshape(B, T, 2 * H, d), pos)
    v = proj[..., 2 * D:].reshape(B, T, H, 2 * d)
    lp = lam_p.astype(F32)
    lam = jnp.exp(jnp.sum(lp[0] * lp[1])) - jnp.exp(jnp.sum(lp[2] * lp[3])) + lam_init
    nb = T // Qb
    qb = jnp.moveaxis(q.reshape(B, nb, Qb, 2 * H, d), 1, 0)
    kf, vf = k.astype(F32), v.astype(F32)
    kpos = jnp.arange(T)

    def block(args):
        q_blk, i = args
        s = jnp.einsum('bqhd,bkhd->bhqk', q_blk.astype(F32), kf)
        qpos = i * Qb + jnp.arange(Qb)
        s = jnp.where(kpos[None, :] <= qpos[:, None], s, NEG_INF)
        p = jax.nn.softmax(s, axis=-1).reshape(B, H, 2, Qb, T)
        attn = p[:, :, 0] - lam * p[:, :, 1]
        return jnp.einsum('bhqk,bkhe->bqhe', attn, vf)

    o = lax.map(block, (qb, jnp.arange(nb)))
    o = jnp.moveaxis(o, 0, 1).reshape(B, T, H, 2 * d)
    o = _rms_norm(o, subln_g, 1e-5) * (1.0 - lam_init)
    return o.reshape(B, T, D).astype(u.dtype) @ w_out


def _swa_sink_mixer(u, pos, w_qkv, b_qkv, sinks, w_out, b_out):
    B, T, _ = u.shape
    Hq, Hkv, d, W = SWA_Q_HEADS, SWA_KV_HEADS, SWA_HEAD_DIM, SWA_WINDOW
    G = Hq // Hkv
    nb = T // W
    proj = u @ w_qkv + b_qkv
    q = _rope(proj[..., :Hq * d].reshape(B, T, Hq, d), pos) * d ** -0.5
    k = _rope(proj[..., Hq * d:(Hq + Hkv) * d].reshape(B, T, Hkv, d), pos)
    v = proj[..., (Hq + Hkv) * d:].reshape(B, T, Hkv, d)
    qb = q.astype(F32).reshape(B, nb, W, Hkv, G, d)

    def band(t):
        tb = t.astype(F32).reshape(B, nb, W, Hkv, d)
        prev = jnp.pad(tb, ((0, 0), (1, 0), (0, 0), (0, 0), (0, 0)))[:, :-1]
        return jnp.concatenate([prev, tb], axis=2)

    kb, vb = band(k), band(v)
    s = jnp.einsum('bnqhgd,bnkhd->bnhgqk', qb, kb)
    qi = jnp.arange(W)[:, None]
    kj = jnp.arange(2 * W)[None, :]
    rel = W + qi - kj
    in_window = (rel >= 0) & (rel < W)
    key_pos = jnp.arange(nb)[:, None] * W + jnp.arange(2 * W)[None, :] - W
    valid = in_window[None, :, :] & (key_pos >= 0)[:, None, :]
    s = jnp.where(valid[None, :, None, None], s, NEG_INF)
    sink = sinks.astype(F32).reshape(Hkv, G)[None, None, :, :, None, None]
    m = jnp.maximum(jnp.max(s, -1, keepdims=True), sink)
    p = jnp.exp(s - m)
    p = p / (jnp.sum(p, -1, keepdims=True) + jnp.exp(sink - m))
    o = jnp.einsum('bnhgqk,bnkhd->bnqhgd', p, vb).reshape(B, T, Hq * d)
    return o.astype(u.dtype) @ w_out + b_out


def _count(m):
    return len(range(m, DEPTH, N_MIXERS))


def setup_inputs(seed: int = 0) -> dict:
    key = jax.random.key(seed)
    keys = iter(jax.random.split(key, 48))

    def nrm(shape, std):
        return jax.random.normal(next(keys), shape, F32) * std

    def unif(shape, lo, hi):
        return jax.random.uniform(next(keys), shape, F32, lo, hi)

    D = D_MODEL
    nA, nB, nC, nD = (_count(m) for m in range(N_MIXERS))
    beta = DEEPNORM_BETA
    gdn_hd = GDN_HEADS * GDN_HEAD_DIM
    swa_hd = SWA_Q_HEADS * SWA_HEAD_DIM
    dt = jnp.exp(unif((nB, GDN_HEADS), math.log(1e-3), math.log(1e-1)))
    return {
        'x': nrm((BATCH, SEQ, D), 1.0),
        'c': nrm((BATCH, D), 1.0),
        'ada_w': nrm((DEPTH, D, 6 * D), 0.1 * D ** -0.5),
        'ada_b': nrm((DEPTH, 6 * D), 0.02),
        'ln_g': 1.0 + nrm((DEPTH, 2, D), 0.02),
        'ln_b': nrm((DEPTH, 2, D), 0.02),
        'ffn_w_in': nrm((DEPTH, D, 2 * D_FF), D ** -0.5),
        'ffn_w_out': nrm((DEPTH, D_FF, D), beta * D_FF ** -0.5),
        'rwkv_mu': unif((nA, 6, D), 0.0, 1.0),
        'rwkv_w_rkv': nrm((nA, 3, D, D), D ** -0.5),
        'rwkv_w0': unif((nA, D), -6.0, 1.0),
        'rwkv_w1': nrm((nA, D, RWKV_LORA_DECAY), D ** -0.5),
        'rwkv_w2': nrm((nA, RWKV_LORA_DECAY, D), 0.5 * RWKV_LORA_DECAY ** -0.5),
        'rwkv_a0': nrm((nA, D), 0.1),
        'rwkv_a1': nrm((nA, D, RWKV_LORA_ICLR), D ** -0.5),
        'rwkv_a2': nrm((nA, RWKV_LORA_ICLR, D), 0.5 * RWKV_LORA_ICLR ** -0.5),
        'rwkv_g1': nrm((nA, D, RWKV_LORA_GATE), D ** -0.5),
        'rwkv_g2': nrm((nA, RWKV_LORA_GATE, D), RWKV_LORA_GATE ** -0.5),
        'rwkv_k_k': 1.0 + nrm((nA, D), 0.1),
        'rwkv_k_a': 1.0 + nrm((nA, D), 0.1),
        'rwkv_r_k': nrm((nA, RWKV_HEADS, RWKV_HEAD_DIM), 0.1),
        'rwkv_gn_g': 1.0 + nrm((nA, D), 0.02),
        'rwkv_gn_b': nrm((nA, D), 0.02),
        'rwkv_w_out': nrm((nA, D, D), beta * D ** -0.5),
        'gdn_w_in': nrm((nB, D, GDN_PROJ), D ** -0.5),
        'gdn_conv': nrm((nB, GDN_CONV, GDN_QKV), GDN_CONV ** -0.5),
        'gdn_a_log': jnp.log(unif((nB, GDN_HEADS), 1.0, 16.0)),
        'gdn_dt_bias': dt + jnp.log(-jnp.expm1(-dt)),
        'gdn_norm_g': 1.0 + nrm((nB, GDN_HEAD_DIM), 0.02),
        'gdn_w_out': nrm((nB, gdn_hd, D), beta * gdn_hd ** -0.5),
        'diff_w_in': nrm((nC, D, 3 * D), D ** -0.5),
        'diff_lambda': nrm((nC, 4, DIFF_HEAD_DIM), 0.1),
        'diff_subln_g': 1.0 + nrm((nC, 2 * DIFF_HEAD_DIM), 0.02),
        'diff_w_out': nrm((nC, D, D), beta * D ** -0.5),
        'swa_w_qkv': nrm((nD, D, SWA_QKV), D ** -0.5),
        'swa_b_qkv': nrm((nD, SWA_QKV), 0.02),
        'swa_sinks': nrm((nD, SWA_Q_HEADS), 1.0),
        'swa_w_out': nrm((nD, swa_hd, D), beta * swa_hd ** -0.5),
        'swa_b_out': nrm((nD, D), 0.02),
    }


def reference(x, c, ada_w, ada_b, ln_g, ln_b, ffn_w_in, ffn_w_out,
              rwkv_mu, rwkv_w_rkv, rwkv_w0, rwkv_w1, rwkv_w2, rwkv_a0, rwkv_a1, rwkv_a2,
              rwkv_g1, rwkv_g2, rwkv_k_k, rwkv_k_a, rwkv_r_k, rwkv_gn_g, rwkv_gn_b, rwkv_w_out,
              gdn_w_in, gdn_conv, gdn_a_log, gdn_dt_bias, gdn_norm_g, gdn_w_out,
              diff_w_in, diff_lambda, diff_subln_g, diff_w_out,
              swa_w_qkv, swa_b_qkv, swa_sinks, swa_w_out, swa_b_out):
    T = x.shape[1]
    pos = jnp.arange(T)
    cond = jax.nn.silu(c)
    for i in range(DEPTH):
        mod = (cond @ ada_w[i] + ada_b[i])[:, None, :]
        sh1, sc1, ga1, sh2, sc2, ga2 = jnp.split(mod, 6, axis=-1)
        u = x * (1 + sc1) + sh1
        m, j = i % N_MIXERS, i // N_MIXERS
        if m == 0:
            y = _rwkv7_mixer(u, rwkv_mu[j], rwkv_w_rkv[j], rwkv_w0[j], rwkv_w1[j], rwkv_w2[j],
                             rwkv_a0[j], rwkv_a1[j], rwkv_a2[j], rwkv_g1[j], rwkv_g2[j],
                             rwkv_k_k[j], rwkv_k_a[j], rwkv_r_k[j], rwkv_gn_g[j], rwkv_gn_b[j],
                             rwkv_w_out[j])
        elif m == 1:
            y = _gated_deltanet_mixer(u, gdn_w_in[j], gdn_conv[j], gdn_a_log[j], gdn_dt_bias[j],
                                      gdn_norm_g[j], gdn_w_out[j])
        elif m == 2:
            lam_init = 0.8 - 0.6 * math.exp(-0.3 * i)
            y = _diff_attention_mixer(u, pos, diff_w_in[j], diff_lambda[j], diff_subln_g[j],
                                      diff_w_out[j], lam_init)
        else:
            y = _swa_sink_mixer(u, pos, swa_w_qkv[j], swa_b_qkv[j], swa_sinks[j], swa_w_out[j],
                                swa_b_out[j])
        x = _layer_norm(DEEPNORM_ALPHA * x + (1 + ga1) * y, ln_g[i, 0], ln_b[i, 0])
        u = x * (1 + sc2) + sh2
        gate, up = jnp.split(u @ ffn_w_in[i], 2, axis=-1)
        y = (jax.nn.silu(gate) * up) @ ffn_w_out[i]
        x = _layer_norm(DEEPNORM_ALPHA * x + (1 + ga2) * y, ln_g[i, 1], ln_b[i, 1])
    return x
```

```python
import functools
import math

import jax
import jax.numpy as jnp
from jax import lax
from jax.experimental import pallas as pl
from jax.experimental.pallas import tpu as pltpu

F32 = jnp.float32
BF16 = jnp.bfloat16

N_MIXERS = 4
LN_EPS = 1e-5
ROPE_THETA = 10000.0
NEG_INF = -1e30

RWKV_HEAD_DIM = 64
RWKV_GN_EPS = 64e-5
GDN_HEAD_DIM = 128
GDN_CONV = 4
DIFF_HEAD_DIM = 64
SWA_HEAD_DIM = 64
SWA_KV_HEADS = 2
SWA_WINDOW = 128

LANES = 128
SUBLANES = 8
CHUNK = 64
VMEM_LIMIT = 56 * 1024 * 1024

_NN = (((1,), (0,)), ((), ()))
_NT = (((1,), (1,)), ((), ()))
_TN = (((0,), (0,)), ((), ()))


def _dot1(a, b, dims=_NN):
    return lax.dot_general(a.astype(BF16), b.astype(BF16), dims, preferred_element_type=F32)


def _split(a):
    hi = a.astype(BF16)
    lo = (a - hi.astype(F32)).astype(BF16)
    return hi, lo


def _dot3(a, b, dims=_NN):
    ah, al = _split(a)
    bh, bl = _split(b)
    d = lambda x, y: lax.dot_general(x, y, dims, preferred_element_type=F32)
    return d(ah, bh) + (d(ah, bl) + d(al, bh))


def _dotx(a, b, dims=_NN):
    return lax.dot_general(a, b, dims, precision=lax.Precision.HIGHEST, preferred_element_type=F32)


def _params(*sem):
    return pltpu.CompilerParams(dimension_semantics=sem, vmem_limit_bytes=VMEM_LIMIT)


def _silu(x):
    return x * jax.nn.sigmoid(x)


def _softplus(x):
    return jnp.maximum(x, 0.0) + jnp.log(1.0 + jnp.exp(-jnp.abs(x)))


def _layer_norm_rows(h, g, b):
    mu = jnp.mean(h, -1, keepdims=True)
    d = h - mu
    var = jnp.mean(d * d, -1, keepdims=True)
    return d * lax.rsqrt(var + LN_EPS) * g + b


def _inv_unit_lower(n, c):
    ri = lax.broadcasted_iota(jnp.int32, (c, c), 0)
    ci = lax.broadcasted_iota(jnp.int32, (c, c), 1)
    t = jnp.where(ri == ci, 1.0, 0.0).astype(F32) + n
    p = n
    k = 1
    while k < c // 2:
        p = _dot3(p, p)
        t = t + _dot3(t, p)
        k *= 2
    return t


def _mod_kernel(c_ref, w_ref, b_ref, o_ref):
    c = c_ref[...]
    o_ref[0] = _dot3(_silu(c), w_ref[0]) + b_ref[0]


def _adaln_mod(c, ada_w, ada_b):
    depth, d, m = ada_w.shape
    b = c.shape[0]
    cp = jnp.pad(c, ((0, SUBLANES - b), (0, 0)))
    tn = 1024
    out = pl.pallas_call(
        _mod_kernel,
        grid=(depth, m // tn),
        in_specs=[pl.BlockSpec((SUBLANES, d), lambda l, j: (0, 0)),
                  pl.BlockSpec((1, d, tn), lambda l, j: (l, 0, j)),
                  pl.BlockSpec((1, 1, tn), lambda l, j: (l, 0, j))],
        out_specs=pl.BlockSpec((1, SUBLANES, tn), lambda l, j: (l, 0, j)),
        out_shape=jax.ShapeDtypeStruct((depth, SUBLANES, m), F32),
        compiler_params=_params("parallel", "parallel"),
        name="adaln_mod",
    )(cp, ada_w, ada_b.reshape(depth, 1, m))
    return out[:, :b]


def _proj_kernel(*refs, rope_tiles, tn):
    if rope_tiles:
        x_ref, sc_ref, sh_ref, w_ref, b_ref, cs_ref, cos_ref, sin_ref, o_ref, u_scr = refs
    else:
        x_ref, sc_ref, sh_ref, w_ref, b_ref, cs_ref, o_ref, u_scr = refs
    j = pl.program_id(2)

    @pl.when(j == 0)
    def _():
        u_scr[...] = (x_ref[0] * (1.0 + sc_ref[0]) + sh_ref[0]).astype(BF16)

    acc = jnp.dot(u_scr[...], w_ref[...], preferred_element_type=F32) + b_ref[...]
    acc = acc * cs_ref[...]
    if rope_tiles:
        @pl.when(j < rope_tiles)
        def _():
            lane = lax.broadcasted_iota(jnp.int32, acc.shape, 1)
            first = (lane & 63) < 32
            rot = jnp.where(first, pltpu.roll(acc, tn - 32, 1), pltpu.roll(acc, 32, 1))
            o_ref[0] = (acc * cos_ref[...] + rot * sin_ref[...]).astype(o_ref.dtype)

        @pl.when(j >= rope_tiles)
        def _():
            o_ref[0] = acc.astype(o_ref.dtype)
    else:
        o_ref[0] = acc.astype(o_ref.dtype)


def _proj(x, sc, sh, w, bias, colscale, *, tn, out_dtype, rope=None, rope_cols=0):
    b, t, d = x.shape
    m = w.shape[1]
    tm = min(512, t)
    rope_tiles = rope_cols // tn
    in_specs = [pl.BlockSpec((1, tm, d), lambda bi, i, j: (bi, i, 0)),
                pl.BlockSpec((1, 1, d), lambda bi, i, j: (bi, 0, 0)),
                pl.BlockSpec((1, 1, d), lambda bi, i, j: (bi, 0, 0)),
                pl.BlockSpec((d, tn), lambda bi, i, j: (0, j)),
                pl.BlockSpec((1, tn), lambda bi, i, j: (0, j)),
                pl.BlockSpec((1, tn), lambda bi, i, j: (0, j))]
    args = [x, sc, sh, w, bias, colscale]
    if rope_tiles:
        cos, sin = rope
        in_specs += [pl.BlockSpec((tm, tn), lambda bi, i, j: (i, 0)),
                     pl.BlockSpec((tm, tn), lambda bi, i, j: (i, 0))]
        args += [cos, sin]
    return pl.pallas_call(
        functools.partial(_proj_kernel, rope_tiles=rope_tiles, tn=tn),
        grid=(b, t // tm, m // tn),
        in_specs=in_specs,
        out_specs=pl.BlockSpec((1, tm, tn), lambda bi, i, j: (bi, i, j)),
        out_shape=jax.ShapeDtypeStruct((b, t, m), out_dtype),
        scratch_shapes=[pltpu.VMEM((tm, d), BF16)],
        compiler_params=_params("parallel", "parallel", "arbitrary"),
        name="proj",
    )(*args)


def _rope_tables(t, width):
    d = 64
    inv = ROPE_THETA ** (-jnp.arange(0, d, 2, dtype=F32) / d)
    ang = jnp.arange(t, dtype=F32)[:, None] * inv[None, :]
    cos = jnp.concatenate([jnp.cos(ang), jnp.cos(ang)], -1)
    sin = jnp.concatenate([-jnp.sin(ang), jnp.sin(ang)], -1)
    reps = width // d
    return jnp.tile(cos, (1, reps)), jnp.tile(sin, (1, reps))


def _outproj_ln_kernel(a_ref, w_ref, b_ref, x_ref, ga_ref, g_ref, be_ref, o_ref, *, alpha):
    y = jnp.dot(a_ref[0].astype(BF16), w_ref[...], preferred_element_type=F32) + b_ref[...]
    h = alpha * x_ref[0] + (1.0 + ga_ref[0]) * y
    o_ref[0] = _layer_norm_rows(h, g_ref[...], be_ref[...])


def _outproj_ln(a, w, bias, x, ga, ln_g, ln_b, alpha):
    b, t, k = a.shape
    d = w.shape[1]
    tm = min(512, t)
    return pl.pallas_call(
        functools.partial(_outproj_ln_kernel, alpha=alpha),
        grid=(b, t // tm),
        in_specs=[pl.BlockSpec((1, tm, k), lambda bi, i: (bi, i, 0)),
                  pl.BlockSpec((k, d), lambda bi, i: (0, 0)),
                  pl.BlockSpec((1, d), lambda bi, i: (0, 0)),
                  pl.BlockSpec((1, tm, d), lambda bi, i: (bi, i, 0)),
                  pl.BlockSpec((1, 1, d), lambda bi, i: (bi, 0, 0)),
                  pl.BlockSpec((1, d), lambda bi, i: (0, 0)),
                  pl.BlockSpec((1, d), lambda bi, i: (0, 0))],
        out_specs=pl.BlockSpec((1, tm, d), lambda bi, i: (bi, i, 0)),
        out_shape=jax.ShapeDtypeStruct((b, t, d), F32),
        compiler_params=_params("parallel", "parallel"),
        name="outproj_ln",
    )(a, w, bias, x, ga, ln_g, ln_b)


def _ffn_kernel(x_ref, sc_ref, sh_ref, ga_ref, wg_ref, wu_ref, wo_ref, g_ref, be_ref, o_ref,
                u_scr, acc_scr, *, alpha):
    j = pl.program_id(2)

    @pl.when(j == 0)
    def _():
        u_scr[...] = (x_ref[0] * (1.0 + sc_ref[0]) + sh_ref[0]).astype(BF16)
        acc_scr[...] = jnp.zeros_like(acc_scr)

    u = u_scr[...]
    hg = jnp.dot(u, wg_ref[...], preferred_element_type=F32)
    hu = jnp.dot(u, wu_ref[...], preferred_element_type=F32)
    act = (_silu(hg) * hu).astype(BF16)
    acc_scr[...] += jnp.dot(act, wo_ref[...], preferred_element_type=F32)

    @pl.when(j == pl.num_programs(2) - 1)
    def _():
        h = alpha * x_ref[0] + (1.0 + ga_ref[0]) * acc_scr[...]
        o_ref[0] = _layer_norm_rows(h, g_ref[...], be_ref[...])


def _ffn(x, sc, sh, ga, w_in, w_out, ln_g, ln_b, alpha):
    b, t, d = x.shape
    f = w_out.shape[0]
    tm = min(512, t)
    tf = f // 2 if (f // 2) % LANES == 0 else f
    nf = f // tf
    return pl.pallas_call(
        functools.partial(_ffn_kernel, alpha=alpha),
        grid=(b, t // tm, nf),
        in_specs=[pl.BlockSpec((1, tm, d), lambda bi, i, j: (bi, i, 0)),
                  pl.BlockSpec((1, 1, d), lambda bi, i, j: (bi, 0, 0)),
                  pl.BlockSpec((1, 1, d), lambda bi, i, j: (bi, 0, 0)),
                  pl.BlockSpec((1, 1, d), lambda bi, i, j: (bi, 0, 0)),
                  pl.BlockSpec((d, tf), lambda bi, i, j: (0, j)),
                  pl.BlockSpec((d, tf), lambda bi, i, j: (0, j + nf)),
                  pl.BlockSpec((tf, d), lambda bi, i, j: (j, 0)),
                  pl.BlockSpec((1, d), lambda bi, i, j: (0, 0)),
                  pl.BlockSpec((1, d), lambda bi, i, j: (0, 0))],
        out_specs=pl.BlockSpec((1, tm, d), lambda bi, i, j: (bi, i, 0)),
        out_shape=jax.ShapeDtypeStruct((b, t, d), F32),
        scratch_shapes=[pltpu.VMEM((tm, d), BF16), pltpu.VMEM((tm, d), F32)],
        compiler_params=_params("parallel", "parallel", "arbitrary"),
        name="ffn",
    )(x, sc, sh, ga, w_in, w_in, w_out, ln_g, ln_b)


def _swa_kernel(sink_ref, q_ref, kc_ref, kp_ref, vc_ref, vp_ref, o_ref, *, w, hq, hkv, d):
    i = pl.program_id(1)
    g = hq // hkv
    qi = lax.broadcasted_iota(jnp.int32, (w, 2 * w), 0)
    kj = lax.broadcasted_iota(jnp.int32, (w, 2 * w), 1)
    rel = w + qi - kj
    first_key = jnp.where(i > 0, 0, w)
    valid = (rel >= 0) & (rel < w) & (kj >= first_key)
    for hk in range(hkv):
        ks = pl.ds(hk * d, d)
        kcat = jnp.concatenate([kp_ref[0, :, ks], kc_ref[0, :, ks]], axis=0)
        vcat = jnp.concatenate([vp_ref[0, :, ks], vc_ref[0, :, ks]], axis=0)
        for gi in range(g):
            h = hk * g + gi
            q = q_ref[0, :, pl.ds(h * d, d)]
            s = lax.dot_general(q, kcat, _NT, preferred_element_type=F32)
            s = jnp.where(valid, s, NEG_INF)
            sink = sink_ref[h]
            m = jnp.maximum(jnp.max(s, -1, keepdims=True), sink)
            p = jnp.exp(s - m)
            denom = jnp.sum(p, -1, keepdims=True) + jnp.exp(sink - m)
            o = jnp.dot(p.astype(BF16), vcat, preferred_element_type=F32) / denom
            o_ref[0, :, pl.ds(h * d, d)] = o.astype(o_ref.dtype)


def _swa_attention(qkv, sinks, hq):
    b, t, _ = qkv.shape
    w, d, hkv = SWA_WINDOW, SWA_HEAD_DIM, SWA_KV_HEADS
    kblk = hq * d // LANES
    vblk = kblk + 1
    return pl.pallas_call(
        functools.partial(_swa_kernel, w=w, hq=hq, hkv=hkv, d=d),
        grid=(b, t // w),
        in_specs=[pl.BlockSpec(memory_space=pltpu.SMEM),
                  pl.BlockSpec((1, w, hq * d), lambda bi, i: (bi, i, 0)),
                  pl.BlockSpec((1, w, LANES), lambda bi, i: (bi, i, kblk)),
                  pl.BlockSpec((1, w, LANES), lambda bi, i: (bi, jnp.maximum(i - 1, 0), kblk)),
                  pl.BlockSpec((1, w, LANES), lambda bi, i: (bi, i, vblk)),
                  pl.BlockSpec((1, w, LANES), lambda bi, i: (bi, jnp.maximum(i - 1, 0), vblk))],
        out_specs=pl.BlockSpec((1, w, hq * d), lambda bi, i: (bi, i, 0)),
        out_shape=jax.ShapeDtypeStruct((b, t, hq * d), BF16),
        compiler_params=_params("parallel", "parallel"),
        name="swa_attn",
    )(sinks, qkv, qkv, qkv, qkv, qkv)


def _diff_kernel(q_ref, k_ref, v_ref, lam_ref, g_ref, o_ref, m_scr, l_scr, acc_scr, *, tq, lam_init):
    i = pl.program_id(2)
    q = q_ref[0]
    lane = lax.broadcasted_iota(jnp.int32, q.shape, 1)
    zero = jnp.zeros_like(q)
    qs = (jnp.where(lane < DIFF_HEAD_DIM, q, zero), jnp.where(lane >= DIFF_HEAD_DIM, q, zero))
    m_scr[...] = jnp.full_like(m_scr, NEG_INF)
    l_scr[...] = jnp.zeros_like(l_scr)
    acc_scr[...] = jnp.zeros_like(acc_scr)

    def step(j, masked):
        start = pl.multiple_of(j * tq, tq)
        kb = k_ref[0, pl.ds(start, tq), :]
        vb = v_ref[0, pl.ds(start, tq), :]
        for c in range(2):
            s = lax.dot_general(qs[c], kb, _NT, preferred_element_type=F32)
            if masked:
                row = lax.broadcasted_iota(jnp.int32, s.shape, 0)
                col = lax.broadcasted_iota(jnp.int32, s.shape, 1)
                s = jnp.where(col <= row, s, NEG_INF)
            m_old = m_scr[c]
            m_new = jnp.maximum(m_old, jnp.max(s, -1, keepdims=True))
            p = jnp.exp(s - m_new)
            corr = jnp.exp(m_old - m_new)
            l_scr[c] = corr * l_scr[c] + jnp.sum(p, -1, keepdims=True)
            acc_scr[c] = corr * acc_scr[c] + jnp.dot(p.astype(BF16), vb, preferred_element_type=F32)
            m_scr[c] = m_new

    def body(j, carry):
        step(j, False)
        return carry

    lax.fori_loop(0, i, body, 0)
    step(i, True)

    lp = lam_ref[...]
    lam = (jnp.exp(jnp.sum(lp[0:1] * lp[1:2], -1, keepdims=True))
           - jnp.exp(jnp.sum(lp[2:3] * lp[3:4], -1, keepdims=True)) + lam_init)
    o = acc_scr[0] / l_scr[0] - lam * (acc_scr[1] / l_scr[1])
    o = o * lax.rsqrt(jnp.mean(o * o, -1, keepdims=True) + 1e-5) * g_ref[...]
    o_ref[0] = (o * (1.0 - lam_init)).astype(o_ref.dtype)


def _diff_attention(qkv, lam_p, subln_g, heads, lam_init):
    b, t, _ = qkv.shape
    tq = min(512, t)
    return pl.pallas_call(
        functools.partial(_diff_kernel, tq=tq, lam_init=lam_init),
        grid=(b, heads, t // tq),
        in_specs=[pl.BlockSpec((1, tq, LANES), lambda bi, h, i: (bi, i, h)),
                  pl.BlockSpec((1, t, LANES), lambda bi, h, i: (bi, 0, heads + h)),
                  pl.BlockSpec((1, t, LANES), lambda bi, h, i: (bi, 0, 2 * heads + h)),
                  pl.BlockSpec(lam_p.shape, lambda bi, h, i: (0, 0)),
                  pl.BlockSpec((1, LANES), lambda bi, h, i: (0, 0))],
        out_specs=pl.BlockSpec((1, tq, LANES), lambda bi, h, i: (bi, i, h)),
        out_shape=jax.ShapeDtypeStruct((b, t, heads * LANES), BF16),
        scratch_shapes=[pltpu.VMEM((2, tq, 1), F32), pltpu.VMEM((2, tq, 1), F32),
                        pltpu.VMEM((2, tq, LANES), F32)],
        compiler_params=_params("parallel", "parallel", "arbitrary"),
        name="diff_attn",
    )(qkv, qkv, qkv, lam_p, subln_g)


def _gdn_prep_kernel(alog_ref, dtb_ref, q_ref, k_ref, v_ref, qh_ref, kh_ref, vh_ref, wq_ref, wk_ref, wv_ref,
                     ba_ref, qo_ref, ko_ref, vo_ref, go_ref, bo_ref, buf, *, tm, heads):
    i = pl.program_id(1)
    h = pl.program_id(2)

    def conv(cur_ref, halo_ref, w_ref):
        buf[0:SUBLANES, :] = jnp.where(i > 0, halo_ref[0], 0.0)
        buf[SUBLANES:, :] = cur_ref[0]
        acc = None
        for kk in range(GDN_CONV):
            term = buf[pl.ds(SUBLANES - (GDN_CONV - 1) + kk, tm), :] * w_ref[kk:kk + 1, :]
            acc = term if acc is None else acc + term
        return _silu(acc)

    def l2n(x):
        return x * lax.rsqrt(jnp.sum(x * x, -1, keepdims=True) + 1e-6)

    qo_ref[0] = l2n(conv(q_ref, qh_ref, wq_ref)) * (GDN_HEAD_DIM ** -0.5)
    ko_ref[0] = l2n(conv(k_ref, kh_ref, wk_ref))
    vo_ref[0] = conv(v_ref, vh_ref, wv_ref)
    ba = ba_ref[0]
    lane = lax.broadcasted_iota(jnp.int32, ba.shape, 1)
    b_in = jnp.sum(jnp.where(lane == h, ba, 0.0), -1, keepdims=True)
    a_in = jnp.sum(jnp.where(lane == h + heads, ba, 0.0), -1, keepdims=True)
    beta = jax.nn.sigmoid(b_in)
    a_log = jnp.full((1, LANES), alog_ref[h], F32)
    dt_bias = jnp.full((1, LANES), dtb_ref[h], F32)
    bo_ref[0] = jnp.broadcast_to(beta, (tm, LANES))
    go_ref[0] = -jnp.exp(a_log) * _softplus(a_in + dt_bias)


def _gdn_prep(proj, ba, conv_w, a_log, dt_bias, heads):
    b, t, _ = proj.shape
    tm = min(512, t)
    hb = tm // SUBLANES
    cur = lambda off: pl.BlockSpec((1, tm, LANES), lambda bi, i, h: (bi, i, off + h))
    halo = lambda off: pl.BlockSpec((1, SUBLANES, LANES),
                                    lambda bi, i, h: (bi, jnp.maximum(i * hb - 1, 0), off + h))
    cw = lambda off: pl.BlockSpec((GDN_CONV, LANES), lambda bi, i, h: (0, off + h))
    out = jax.ShapeDtypeStruct((b, t, heads * LANES), F32)
    ospec = pl.BlockSpec((1, tm, LANES), lambda bi, i, h: (bi, i, h))
    return pl.pallas_call(
        functools.partial(_gdn_prep_kernel, tm=tm, heads=heads),
        grid=(b, t // tm, heads),
        in_specs=[pl.BlockSpec(memory_space=pltpu.SMEM), pl.BlockSpec(memory_space=pltpu.SMEM),
                  cur(0), cur(heads), cur(2 * heads), halo(0), halo(heads), halo(2 * heads),
                  cw(0), cw(heads), cw(2 * heads),
                  pl.BlockSpec((1, tm, LANES), lambda bi, i, h: (bi, i, 0))],
        out_specs=[ospec] * 5,
        out_shape=[out] * 5,
        scratch_shapes=[pltpu.VMEM((tm + SUBLANES, LANES), F32)],
        compiler_params=_params("parallel", "parallel", "arbitrary"),
        name="gdn_prep",
    )(a_log, dt_bias, proj, proj, proj, proj, proj, proj, conv_w, conv_w, conv_w, ba)


def _gdn_scan_kernel(q_ref, k_ref, v_ref, g_ref, beta_ref, z_ref, ng_ref, o_ref, s_scr, *, c, cb):
    @pl.when(pl.program_id(2) == 0)
    def _():
        s_scr[...] = jnp.zeros_like(s_scr)

    ri = lax.broadcasted_iota(jnp.int32, (c, c), 0)
    ci = lax.broadcasted_iota(jnp.int32, (c, c), 1)
    causal = ri >= ci
    strict = ri > ci
    tril1 = jnp.where(causal, 1.0, 0.0).astype(F32)
    ones = jnp.ones((c, LANES), F32)
    lane0 = lax.broadcasted_iota(jnp.int32, (c, LANES), 1) == 0
    for s in range(cb // c):
        sl = pl.ds(s * c, c)
        q, k, v = q_ref[0, sl, :], k_ref[0, sl, :], v_ref[0, sl, :]
        beta = beta_ref[0, sl, :]
        gc = _dotx(tril1, g_ref[0, sl, :])
        gc_row = _dotx(ones, jnp.where(lane0, gc, 0.0), _NT)
        decay = jnp.where(causal, jnp.exp(jnp.minimum(gc[:, :c] - gc_row, 0.0)), 0.0)
        kb, vb = k * beta, v * beta
        lmat = jnp.where(strict, _dot1(kb, k, _NT) * decay, 0.0)
        tinv = _inv_unit_lower(-lmat, c)
        eg = jnp.exp(gc)
        u_c = _dot1(tinv, vb)
        w_c = _dot1(tinv, kb * eg)
        a_in = jnp.where(causal, _dot1(q, k, _NT) * decay, 0.0)
        g_last = gc[c - 1:c, :]
        k_dec = k * jnp.exp(g_last - gc)
        st = s_scr[...]
        v_new = u_c - _dot1(w_c, st)
        o = _dot1(q * eg, st) + _dot1(a_in, v_new)
        s_scr[...] = st * jnp.exp(g_last) + _dot1(k_dec, v_new, _TN)
        o = o * lax.rsqrt(jnp.mean(o * o, -1, keepdims=True) + 1e-6) * ng_ref[...]
        o_ref[0, sl, :] = (o * _silu(z_ref[0, sl, :])).astype(o_ref.dtype)


def _gdn_scan(q, k, v, g, beta, proj, norm_g, heads):
    b, t, _ = q.shape
    cb = min(256, t)
    blk = pl.BlockSpec((1, cb, LANES), lambda bi, h, ci: (bi, ci, h))
    return pl.pallas_call(
        functools.partial(_gdn_scan_kernel, c=CHUNK, cb=cb),
        grid=(b, heads, t // cb),
        in_specs=[blk, blk, blk, blk, blk,
                  pl.BlockSpec((1, cb, LANES), lambda bi, h, ci: (bi, ci, 3 * heads + h)),
                  pl.BlockSpec((1, LANES), lambda bi, h, ci: (0, 0))],
        out_specs=blk,
        out_shape=jax.ShapeDtypeStruct((b, t, heads * LANES), BF16),
        scratch_shapes=[pltpu.VMEM((LANES, LANES), F32)],
        compiler_params=_params("parallel", "parallel", "arbitrary"),
        name="gdn_scan",
    )(q, k, v, g, beta, proj, norm_g)


def _rwkv_proj_kernel(x_ref, halo_ref, sc_ref, sh_ref, mu_ref, wrkv_ref, w0_ref, w1_ref, w2_ref, a0_ref, a1_ref,
                      a2_ref, g1_ref, g2_ref, r_ref, k_ref, v_ref, lw_ref, a_ref, g_ref, buf, *, tm):
    i = pl.program_id(1)
    sc1 = 1.0 + sc_ref[0]
    sh = sh_ref[0]
    u = x_ref[0] * sc1 + sh
    buf[0:SUBLANES, :] = jnp.where(i > 0, halo_ref[0] * sc1 + sh, 0.0)
    buf[SUBLANES:, :] = u
    xx = buf[pl.ds(SUBLANES - 1, tm), :] - u
    lerp = lambda n: (u + xx * mu_ref[n:n + 1, :]).astype(BF16)
    dot = lambda a, w: jnp.dot(a, w, preferred_element_type=F32)
    r_ref[0] = dot(lerp(0), wrkv_ref[0])
    k_ref[0] = dot(lerp(1), wrkv_ref[1])
    v_ref[0] = dot(lerp(2), wrkv_ref[2])
    hw = jnp.tanh(dot(lerp(3), w1_ref[...])).astype(BF16)
    w_log = -_softplus(-(w0_ref[...] + dot(hw, w2_ref[...]))) - 0.5
    lw_ref[0] = -jnp.exp(w_log)
    ha = dot(lerp(4), a1_ref[...]).astype(BF16)
    a_ref[0] = jax.nn.sigmoid(a0_ref[...] + dot(ha, a2_ref[...]))
    hg = jax.nn.sigmoid(dot(lerp(5), g1_ref[...])).astype(BF16)
    g_ref[0] = dot(hg, g2_ref[...])


def _rwkv_proj(x, sc, sh, mu, w_rkv, w0, w1, w2, a0, a1, a2, g1, g2):
    b, t, d = x.shape
    tm = min(256, t)
    hb = tm // SUBLANES
    full = lambda a: pl.BlockSpec(a.shape, lambda bi, i: (0,) * a.ndim)
    row = pl.BlockSpec((1, tm, d), lambda bi, i: (bi, i, 0))
    mod = pl.BlockSpec((1, 1, d), lambda bi, i: (bi, 0, 0))
    out = jax.ShapeDtypeStruct((b, t, d), F32)
    consts = [mu, w_rkv, w0, w1, w2, a0, a1, a2, g1, g2]
    return pl.pallas_call(
        functools.partial(_rwkv_proj_kernel, tm=tm),
        grid=(b, t // tm),
        in_specs=[row, pl.BlockSpec((1, SUBLANES, d), lambda bi, i: (bi, jnp.maximum(i * hb - 1, 0), 0)),
                  mod, mod] + [full(a) for a in consts],
        out_specs=[row] * 6,
        out_shape=[out] * 6,
        scratch_shapes=[pltpu.VMEM((tm + SUBLANES, d), F32)],
        compiler_params=_params("parallel", "arbitrary"),
        name="rwkv_proj",
    )(x, x, sc, sh, *consts)


def _rwkv_scan_kernel(r_ref, k_ref, v_ref, lw_ref, a_ref, g_ref, kk_ref, ka_ref, rk_ref, gg_ref, gb_ref,
                      o_ref, h_scr, *, c, cb):
    n = RWKV_HEAD_DIM

    @pl.when(pl.program_id(2) == 0)
    def _():
        h_scr[...] = jnp.zeros_like(h_scr)

    ri = lax.broadcasted_iota(jnp.int32, (c, c), 0)
    ci = lax.broadcasted_iota(jnp.int32, (c, c), 1)
    incl = ri >= ci
    strict = ri > ci
    tril1 = jnp.where(incl, 1.0, 0.0).astype(F32)
    lane = lax.broadcasted_iota(jnp.int32, (c, LANES), 1)
    head0 = lane < n
    masks = (head0, jnp.logical_not(head0))
    hr = lax.broadcasted_iota(jnp.int32, (LANES, LANES), 0)
    hc = lax.broadcasted_iota(jnp.int32, (LANES, LANES), 1)
    blockdiag = (hr < n) == (hc < n)
    pick = lambda x0, x1: jnp.where(head0, x0, x1)
    hsum = lambda x, m: jnp.sum(jnp.where(m, x, 0.0), -1, keepdims=True)

    for s in range(cb // c):
        sl = pl.ds(s * c, c)
        r, k, v = r_ref[0, sl, :], k_ref[0, sl, :], v_ref[0, sl, :]
        lw, a, g = lw_ref[0, sl, :], a_ref[0, sl, :], g_ref[0, sl, :]
        kkr = k * kk_ref[...]
        sq = kkr * kkr
        kk = kkr * pick(lax.rsqrt(hsum(sq, masks[0]) + 1e-6), lax.rsqrt(hsum(sq, masks[1]) + 1e-6))
        k2 = k * (1.0 + (a - 1.0) * ka_ref[...])
        av, bv = -kk, kk * a
        lc = _dotx(tril1, lw)
        lend = lc[c - 1:c, :]
        at = av * jnp.exp(lc - lw)
        rt = r * jnp.exp(lc)
        einv = jnp.exp(-lc)
        bt, kt = bv * einv, k2 * einv
        eend = jnp.exp(lend - lc)
        bh, kh = bv * eend, k2 * eend
        w_p, u0_p, y0_p, arb = [], [], [], []
        for m in masks:
            at_h = jnp.where(m, at, 0.0)
            rt_h = jnp.where(m, rt, 0.0)
            a_ab = jnp.where(strict, _dot1(at_h, bt, _NT), 0.0)
            a_ak = jnp.where(strict, _dot1(at_h, kt, _NT), 0.0)
            a_rb = jnp.where(incl, _dot1(rt_h, bt, _NT), 0.0)
            a_rk = jnp.where(incl, _dot1(rt_h, kt, _NT), 0.0)
            t_h = _inv_unit_lower(a_ab, c)
            w_p.append(_dot1(t_h, at))
            u0_p.append(_dot1(t_h, _dot1(a_ak, v)))
            y0_p.append(_dot1(a_rk, v))
            arb.append(a_rb)
        ht = h_scr[...]
        u = _dot1(pick(*w_p), ht, _NT) + pick(*u0_p)
        y = _dot1(rt, ht, _NT) + pick(_dot1(arb[0], u), _dot1(arb[1], u)) + pick(*y0_p)
        upd = _dot1(u, bh, _TN) + _dot1(v, kh, _TN)
        h_scr[...] = ht * jnp.exp(lend) + jnp.where(blockdiag, upd, 0.0)
        mean = pick(hsum(y, masks[0]), hsum(y, masks[1])) * (1.0 / n)
        yc = y - mean
        var = pick(hsum(yc * yc, masks[0]), hsum(yc * yc, masks[1])) * (1.0 / n)
        yn = yc * lax.rsqrt(var + RWKV_GN_EPS) * gg_ref[...] + gb_ref[...]
        rkk = r * k2 * rk_ref[...]
        bonus = pick(hsum(rkk, masks[0]), hsum(rkk, masks[1])) * v
        o_ref[0, sl, :] = ((yn + bonus) * g).astype(o_ref.dtype)


def _rwkv_scan(r, k, v, lw, a, g, k_k, k_a, r_k, gn_g, gn_b):
    b, t, d = r.shape
    cb = min(256, t)
    blk = pl.BlockSpec((1, cb, LANES), lambda bi, p, ci: (bi, ci, p))
    par = pl.BlockSpec((1, LANES), lambda bi, p, ci: (0, p))
    return pl.pallas_call(
        functools.partial(_rwkv_scan_kernel, c=CHUNK, cb=cb),
        grid=(b, d // LANES, t // cb),
        in_specs=[blk] * 6 + [par] * 5,
        out_specs=blk,
        out_shape=jax.ShapeDtypeStruct((b, t, d), BF16),
        scratch_shapes=[pltpu.VMEM((LANES, LANES), F32)],
        compiler_params=_params("parallel", "parallel", "arbitrary"),
        name="rwkv_scan",
    )(r, k, v, lw, a, g, k_k, k_a, r_k, gn_g, gn_b)


def _row(v):
    return v.reshape(1, -1)


def _rwkv7_mixer(x, sc, sh, mu, w_rkv, w0, w1, w2, a0, a1, a2, g1, g2, k_k, k_a, r_k, gn_g, gn_b):
    r, k, v, lw, a, g = _rwkv_proj(x, sc, sh, mu, w_rkv.astype(BF16), _row(w0), w1.astype(BF16),
                                   w2.astype(BF16), _row(a0), a1.astype(BF16), a2.astype(BF16),
                                   g1.astype(BF16), g2.astype(BF16))
    return _rwkv_scan(r, k, v, lw, a, g, _row(k_k), _row(k_a), _row(r_k), _row(gn_g), _row(gn_b))


def _gdn_mixer(x, sc, sh, w_in, conv_w, a_log, dt_bias, norm_g):
    d = x.shape[-1]
    heads = a_log.shape[0]
    hd = heads * GDN_HEAD_DIM
    w_main = w_in[:, :4 * hd].astype(BF16)
    w_ba = jnp.pad(w_in[:, 4 * hd:], ((0, 0), (0, LANES - 2 * heads))).astype(BF16)
    zeros = lambda m: jnp.zeros((1, m), F32)
    ones = lambda m: jnp.ones((1, m), F32)
    proj = _proj(x, sc, sh, w_main, zeros(4 * hd), ones(4 * hd), tn=512, out_dtype=F32)
    ba = _proj(x, sc, sh, w_ba, zeros(LANES), ones(LANES), tn=LANES, out_dtype=F32)
    q, k, v, g, beta = _gdn_prep(proj, ba, conv_w, a_log, dt_bias, heads)
    return _gdn_scan(q, k, v, g, beta, proj, _row(norm_g), heads)


def _diff_mixer(x, sc, sh, w_in, lam_p, subln_g, lam_init, rope):
    d = x.shape[-1]
    heads = d // (2 * DIFF_HEAD_DIM)
    colscale = jnp.concatenate([jnp.full((1, d), DIFF_HEAD_DIM ** -0.5, F32), jnp.ones((1, 2 * d), F32)], -1)
    qkv = _proj(x, sc, sh, w_in.astype(BF16), jnp.zeros((1, 3 * d), F32), colscale, tn=512, out_dtype=BF16,
                rope=rope(512), rope_cols=2 * d)
    return _diff_attention(qkv, lam_p, _row(subln_g), heads, lam_init)


def _swa_mixer(x, sc, sh, w_qkv, b_qkv, sinks, rope):
    hq = sinks.shape[0]
    qd = hq * SWA_HEAD_DIM
    m = w_qkv.shape[1]
    colscale = jnp.concatenate([jnp.full((1, qd), SWA_HEAD_DIM ** -0.5, F32), jnp.ones((1, m - qd), F32)], -1)
    qkv = _proj(x, sc, sh, w_qkv.astype(BF16), _row(b_qkv), colscale, tn=LANES, out_dtype=BF16,
                rope=rope(LANES), rope_cols=qd + SWA_KV_HEADS * SWA_HEAD_DIM)
    return _swa_attention(qkv, sinks, hq)


def kernel(x, c, ada_w, ada_b, ln_g, ln_b, ffn_w_in, ffn_w_out, rwkv_mu, rwkv_w_rkv, rwkv_w0, rwkv_w1, rwkv_w2, rwkv_a0, rwkv_a1, rwkv_a2, rwkv_g1, rwkv_g2, rwkv_k_k, rwkv_k_a, rwkv_r_k, rwkv_gn_g, rwkv_gn_b, rwkv_w_out, gdn_w_in, gdn_conv, gdn_a_log, gdn_dt_bias, gdn_norm_g, gdn_w_out, diff_w_in, diff_lambda, diff_subln_g, diff_w_out, swa_w_qkv, swa_b_qkv, swa_sinks, swa_w_out, swa_b_out):
    b, t, d = x.shape
    depth = ada_w.shape[0]
    alpha = (2.0 * depth) ** 0.25
    mod = _adaln_mod(c, ada_w, ada_b)
    rope = functools.lru_cache(None)(lambda width: _rope_tables(t, width))
    zero_bias = jnp.zeros((1, d), F32)
    for i in range(depth):
        sh1, sc1, ga1, sh2, sc2, ga2 = (mod[i, :, None, n * d:(n + 1) * d] for n in range(6))
        m, j = i % N_MIXERS, i // N_MIXERS
        if m == 0:
            y = _rwkv7_mixer(x, sc1, sh1, rwkv_mu[j], rwkv_w_rkv[j], rwkv_w0[j], rwkv_w1[j], rwkv_w2[j],
                             rwkv_a0[j], rwkv_a1[j], rwkv_a2[j], rwkv_g1[j], rwkv_g2[j], rwkv_k_k[j],
                             rwkv_k_a[j], rwkv_r_k[j], rwkv_gn_g[j], rwkv_gn_b[j])
            w_out, b_out = rwkv_w_out[j], zero_bias
        elif m == 1:
            y = _gdn_mixer(x, sc1, sh1, gdn_w_in[j], gdn_conv[j], gdn_a_log[j], gdn_dt_bias[j], gdn_norm_g[j])
            w_out, b_out = gdn_w_out[j], zero_bias
        elif m == 2:
            lam_init = 0.8 - 0.6 * math.exp(-0.3 * i)
            y = _diff_mixer(x, sc1, sh1, diff_w_in[j], diff_lambda[j], diff_subln_g[j], lam_init, rope)
            w_out, b_out = diff_w_out[j], zero_bias
        else:
            y = _swa_mixer(x, sc1, sh1, swa_w_qkv[j], swa_b_qkv[j], swa_sinks[j], rope)
            w_out, b_out = swa_w_out[j], _row(swa_b_out[j])
        x = _outproj_ln(y, w_out.astype(BF16), b_out, x, ga1, _row(ln_g[i, 0]), _row(ln_b[i, 0]), alpha)
        x = _ffn(x, sc2, sh2, ga2, ffn_w_in[i].astype(BF16), ffn_w_out[i].astype(BF16),
                 _row(ln_g[i, 1]), _row(ln_b[i, 1]), alpha)
    return x
```

```python
import functools
import math

import jax
import jax.numpy as jnp
from jax import lax
from jax.experimental import pallas as pl
from jax.experimental.pallas import tpu as pltpu

F32 = jnp.float32
BF16 = jnp.bfloat16

N_MIXERS = 4
LN_EPS = 1e-5
ROPE_THETA = 10000.0
NEG_INF = -1e30
LOG2E = math.log2(math.e)

RWKV_HEAD_DIM = 64
RWKV_GN_EPS = 64e-5
GDN_HEAD_DIM = 128
GDN_CONV = 4
DIFF_HEAD_DIM = 64
SWA_HEAD_DIM = 64
SWA_KV_HEADS = 2
SWA_WINDOW = 128

LANES = 128
SUBLANES = 8
CHUNK = 64
VMEM_LIMIT = 56 * 1024 * 1024

_NN = (((1,), (0,)), ((), ()))
_NT = (((1,), (1,)), ((), ()))
_TN = (((0,), (0,)), ((), ()))


def _dot1(a, b, dims=_NN):
    return lax.dot_general(a.astype(BF16), b.astype(BF16), dims, preferred_element_type=F32)


def _split(a):
    hi = a.astype(BF16)
    lo = (a - hi.astype(F32)).astype(BF16)
    return hi, lo


def _dot3(a, b, dims=_NN):
    ah, al = _split(a)
    bh, bl = _split(b)
    d = lambda x, y: lax.dot_general(x, y, dims, preferred_element_type=F32)
    return d(ah, bh) + (d(ah, bl) + d(al, bh))


def _dot_exact01(m01, x, dims=_NN):
    hi = x.astype(BF16)
    r1 = x - hi.astype(F32)
    mid = r1.astype(BF16)
    lo = (r1 - mid.astype(F32)).astype(BF16)
    d = lambda y: lax.dot_general(m01, y, dims, preferred_element_type=F32)
    return d(hi) + (d(mid) + d(lo))


def _params(*sem):
    return pltpu.CompilerParams(dimension_semantics=sem, vmem_limit_bytes=VMEM_LIMIT)


def _silu(x):
    return x * jax.nn.sigmoid(x)


def _softplus(x):
    return jnp.maximum(x, 0.0) + jnp.log(1.0 + jnp.exp(-jnp.abs(x)))


def _layer_norm_rows(h, g, b):
    mu = jnp.mean(h, -1, keepdims=True)
    d = h - mu
    var = jnp.mean(d * d, -1, keepdims=True)
    return d * lax.rsqrt(var + LN_EPS) * g + b


def _inv_unit_lower_many(ns, size, c):
    ri = lax.broadcasted_iota(jnp.int32, (size, size), 0)
    ci = lax.broadcasted_iota(jnp.int32, (size, size), 1)
    eye = jnp.where(ri == ci, 1.0, 0.0).astype(F32)
    ts = [eye + n for n in ns]
    ps = ns
    k = 1
    while k < c // 2:
        ps = [_dot1(p, p) for p in ps]
        ts = [t + _dot1(t, p) for t, p in zip(ts, ps)]
        k *= 2
    return ts


def _mod_kernel(c_ref, w_ref, b_ref, o_ref):
    c = c_ref[...]
    o_ref[0] = _dot3(_silu(c), w_ref[0]) + b_ref[0]


def _adaln_mod(c, ada_w, ada_b):
    depth, d, m = ada_w.shape
    b = c.shape[0]
    cp = jnp.pad(c, ((0, SUBLANES - b), (0, 0)))
    tn = 1024
    out = pl.pallas_call(
        _mod_kernel,
        grid=(depth, m // tn),
        in_specs=[pl.BlockSpec((SUBLANES, d), lambda l, j: (0, 0)),
                  pl.BlockSpec((1, d, tn), lambda l, j: (l, 0, j)),
                  pl.BlockSpec((1, 1, tn), lambda l, j: (l, 0, j))],
        out_specs=pl.BlockSpec((1, SUBLANES, tn), lambda l, j: (l, 0, j)),
        out_shape=jax.ShapeDtypeStruct((depth, SUBLANES, m), F32),
        compiler_params=_params("parallel", "parallel"),
        name="adaln_mod",
    )(cp, ada_w, ada_b.reshape(depth, 1, m))
    return out[:, :b]


def _proj_kernel(*refs, rope_tiles, tn):
    if rope_tiles:
        x_ref, sc_ref, sh_ref, w_ref, b_ref, cs_ref, cos_ref, sin_ref, o_ref, u_scr = refs
    else:
        x_ref, sc_ref, sh_ref, w_ref, b_ref, cs_ref, o_ref, u_scr = refs
    j = pl.program_id(2)

    @pl.when(j == 0)
    def _():
        u_scr[...] = (x_ref[0] * (1.0 + sc_ref[0]) + sh_ref[0]).astype(BF16)

    acc = jnp.dot(u_scr[...], w_ref[...], preferred_element_type=F32) + b_ref[...]
    acc = acc * cs_ref[...]
    if rope_tiles:
        @pl.when(j < rope_tiles)
        def _():
            lane = lax.broadcasted_iota(jnp.int32, acc.shape, 1)
            first = (lane & 63) < 32
            rot = jnp.where(first, pltpu.roll(acc, tn - 32, 1), pltpu.roll(acc, 32, 1))
            o_ref[0] = (acc * cos_ref[...] + rot * sin_ref[...]).astype(o_ref.dtype)

        @pl.when(j >= rope_tiles)
        def _():
            o_ref[0] = acc.astype(o_ref.dtype)
    else:
        o_ref[0] = acc.astype(o_ref.dtype)


def _proj(x, sc, sh, w, bias, colscale, *, tn, out_dtype, rope=None, rope_cols=0):
    b, t, d = x.shape
    m = w.shape[1]
    tm = min(512, t)
    rope_tiles = rope_cols // tn
    in_specs = [pl.BlockSpec((1, tm, d), lambda bi, i, j: (bi, i, 0)),
                pl.BlockSpec((1, 1, d), lambda bi, i, j: (bi, 0, 0)),
                pl.BlockSpec((1, 1, d), lambda bi, i, j: (bi, 0, 0)),
                pl.BlockSpec((d, tn), lambda bi, i, j: (0, j)),
                pl.BlockSpec((1, tn), lambda bi, i, j: (0, j)),
                pl.BlockSpec((1, tn), lambda bi, i, j: (0, j))]
    args = [x, sc, sh, w, bias, colscale]
    if rope_tiles:
        cos, sin = rope
        in_specs += [pl.BlockSpec((tm, tn), lambda bi, i, j: (i, 0)),
                     pl.BlockSpec((tm, tn), lambda bi, i, j: (i, 0))]
        args += [cos, sin]
    return pl.pallas_call(
        functools.partial(_proj_kernel, rope_tiles=rope_tiles, tn=tn),
        grid=(b, t // tm, m // tn),
        in_specs=in_specs,
        out_specs=pl.BlockSpec((1, tm, tn), lambda bi, i, j: (bi, i, j)),
        out_shape=jax.ShapeDtypeStruct((b, t, m), out_dtype),
        scratch_shapes=[pltpu.VMEM((tm, d), BF16)],
        compiler_params=_params("parallel", "parallel", "arbitrary"),
        name="proj",
    )(*args)


def _rope_tables(t, width):
    d = 64
    inv = ROPE_THETA ** (-jnp.arange(0, d, 2, dtype=F32) / d)
    ang = jnp.arange(t, dtype=F32)[:, None] * inv[None, :]
    cos = jnp.concatenate([jnp.cos(ang), jnp.cos(ang)], -1)
    sin = jnp.concatenate([-jnp.sin(ang), jnp.sin(ang)], -1)
    reps = width // d
    return jnp.tile(cos, (1, reps)), jnp.tile(sin, (1, reps))


def _outproj_ln_kernel(a_ref, w_ref, b_ref, x_ref, ga_ref, g_ref, be_ref, o_ref, *, alpha):
    y = jnp.dot(a_ref[0].astype(BF16), w_ref[...], preferred_element_type=F32) + b_ref[...]
    h = alpha * x_ref[0] + (1.0 + ga_ref[0]) * y
    o_ref[0] = _layer_norm_rows(h, g_ref[...], be_ref[...])


def _outproj_ln(a, w, bias, x, ga, ln_g, ln_b, alpha):
    b, t, k = a.shape
    d = w.shape[1]
    tm = min(512, t)
    return pl.pallas_call(
        functools.partial(_outproj_ln_kernel, alpha=alpha),
        grid=(b, t // tm),
        in_specs=[pl.BlockSpec((1, tm, k), lambda bi, i: (bi, i, 0)),
                  pl.BlockSpec((k, d), lambda bi, i: (0, 0)),
                  pl.BlockSpec((1, d), lambda bi, i: (0, 0)),
                  pl.BlockSpec((1, tm, d), lambda bi, i: (bi, i, 0)),
                  pl.BlockSpec((1, 1, d), lambda bi, i: (bi, 0, 0)),
                  pl.BlockSpec((1, d), lambda bi, i: (0, 0)),
                  pl.BlockSpec((1, d), lambda bi, i: (0, 0))],
        out_specs=pl.BlockSpec((1, tm, d), lambda bi, i: (bi, i, 0)),
        out_shape=jax.ShapeDtypeStruct((b, t, d), F32),
        compiler_params=_params("parallel", "parallel"),
        name="outproj_ln",
    )(a, w, bias, x, ga, ln_g, ln_b)


def _ffn_kernel(x_ref, sc_ref, sh_ref, ga_ref, wg_ref, wu_ref, wo_ref, g_ref, be_ref, o_ref,
                u_scr, acc_scr, *, alpha):
    j = pl.program_id(2)

    @pl.when(j == 0)
    def _():
        u_scr[...] = (x_ref[0] * (1.0 + sc_ref[0]) + sh_ref[0]).astype(BF16)
        acc_scr[...] = jnp.zeros_like(acc_scr)

    u = u_scr[...]
    hg = jnp.dot(u, wg_ref[...], preferred_element_type=F32)
    hu = jnp.dot(u, wu_ref[...], preferred_element_type=F32)
    act = (_silu(hg) * hu).astype(BF16)
    acc_scr[...] += jnp.dot(act, wo_ref[...], preferred_element_type=F32)

    @pl.when(j == pl.num_programs(2) - 1)
    def _():
        h = alpha * x_ref[0] + (1.0 + ga_ref[0]) * acc_scr[...]
        o_ref[0] = _layer_norm_rows(h, g_ref[...], be_ref[...])


def _ffn(x, sc, sh, ga, w_in, w_out, ln_g, ln_b, alpha):
    b, t, d = x.shape
    f = w_out.shape[0]
    tm = min(512, t)
    tf = f // 2 if (f // 2) % LANES == 0 else f
    nf = f // tf
    return pl.pallas_call(
        functools.partial(_ffn_kernel, alpha=alpha),
        grid=(b, t // tm, nf),
        in_specs=[pl.BlockSpec((1, tm, d), lambda bi, i, j: (bi, i, 0)),
                  pl.BlockSpec((1, 1, d), lambda bi, i, j: (bi, 0, 0)),
                  pl.BlockSpec((1, 1, d), lambda bi, i, j: (bi, 0, 0)),
                  pl.BlockSpec((1, 1, d), lambda bi, i, j: (bi, 0, 0)),
                  pl.BlockSpec((d, tf), lambda bi, i, j: (0, j)),
                  pl.BlockSpec((d, tf), lambda bi, i, j: (0, j + nf)),
                  pl.BlockSpec((tf, d), lambda bi, i, j: (j, 0)),
                  pl.BlockSpec((1, d), lambda bi, i, j: (0, 0)),
                  pl.BlockSpec((1, d), lambda bi, i, j: (0, 0))],
        out_specs=pl.BlockSpec((1, tm, d), lambda bi, i, j: (bi, i, 0)),
        out_shape=jax.ShapeDtypeStruct((b, t, d), F32),
        scratch_shapes=[pltpu.VMEM((tm, d), BF16), pltpu.VMEM((tm, d), F32)],
        compiler_params=_params("parallel", "parallel", "arbitrary"),
        name="ffn",
    )(x, sc, sh, ga, w_in, w_in, w_out, ln_g, ln_b)


def _swa_kernel(sink_ref, q_ref, kc_ref, kp_ref, vc_ref, vp_ref, o_ref, *, w, hq, hkv, d):
    i = pl.program_id(1)
    g = hq // hkv
    qi = lax.broadcasted_iota(jnp.int32, (w, 2 * w), 0)
    kj = lax.broadcasted_iota(jnp.int32, (w, 2 * w), 1)
    rel = w + qi - kj
    first_key = jnp.where(i > 0, 0, w)
    valid = (rel >= 0) & (rel < w) & (kj >= first_key)
    for hk in range(hkv):
        ks = pl.ds(hk * d, d)
        kcat = jnp.concatenate([kp_ref[0, :, ks], kc_ref[0, :, ks]], axis=0)
        vcat = jnp.concatenate([vp_ref[0, :, ks], vc_ref[0, :, ks]], axis=0)
        for gi in range(g):
            h = hk * g + gi
            q = q_ref[0, :, pl.ds(h * d, d)]
            s = lax.dot_general(q, kcat, _NT, preferred_element_type=F32)
            s = jnp.where(valid, s, NEG_INF)
            sink = sink_ref[h]
            m = jnp.maximum(jnp.max(s, -1, keepdims=True), sink)
            p = jnp.exp(s - m)
            denom = jnp.sum(p, -1, keepdims=True) + jnp.exp(sink - m)
            o = jnp.dot(p.astype(BF16), vcat, preferred_element_type=F32) / denom
            o_ref[0, :, pl.ds(h * d, d)] = o.astype(o_ref.dtype)


def _swa_attention(qkv, sinks, hq):
    b, t, _ = qkv.shape
    w, d, hkv = SWA_WINDOW, SWA_HEAD_DIM, SWA_KV_HEADS
    kblk = hq * d // LANES
    vblk = kblk + 1
    return pl.pallas_call(
        functools.partial(_swa_kernel, w=w, hq=hq, hkv=hkv, d=d),
        grid=(b, t // w),
        in_specs=[pl.BlockSpec(memory_space=pltpu.SMEM),
                  pl.BlockSpec((1, w, hq * d), lambda bi, i: (bi, i, 0)),
                  pl.BlockSpec((1, w, LANES), lambda bi, i: (bi, i, kblk)),
                  pl.BlockSpec((1, w, LANES), lambda bi, i: (bi, jnp.maximum(i - 1, 0), kblk)),
                  pl.BlockSpec((1, w, LANES), lambda bi, i: (bi, i, vblk)),
                  pl.BlockSpec((1, w, LANES), lambda bi, i: (bi, jnp.maximum(i - 1, 0), vblk))],
        out_specs=pl.BlockSpec((1, w, hq * d), lambda bi, i: (bi, i, 0)),
        out_shape=jax.ShapeDtypeStruct((b, t, hq * d), BF16),
        compiler_params=_params("parallel", "parallel"),
        name="swa_attn",
    )(sinks, qkv, qkv, qkv, qkv, qkv)


def _diff_kernel(q_ref, k_ref, v_ref, lam_ref, g_ref, o_ref, q2_scr, m_scr, l_scr, acc_scr, *, tq, lam_init):
    i = pl.program_id(2)
    q = q_ref[0]
    lane = lax.broadcasted_iota(jnp.int32, q.shape, 1)
    zero = jnp.zeros_like(q)
    q2_scr[0:tq, :] = jnp.where(lane < DIFF_HEAD_DIM, q, zero)
    q2_scr[tq:, :] = jnp.where(lane >= DIFF_HEAD_DIM, q, zero)
    m_scr[...] = jnp.full_like(m_scr, NEG_INF)
    l_scr[...] = jnp.zeros_like(l_scr)
    acc_scr[...] = jnp.zeros_like(acc_scr)
    nc = tq // LANES

    def step(j, masked):
        start = pl.multiple_of(j * tq, tq)
        kb = k_ref[0, pl.ds(start, tq), :]
        vb = v_ref[0, pl.ds(start, tq), :]
        s = lax.dot_general(q2_scr[...], kb, _NT, preferred_element_type=F32)
        if masked:
            row = lax.broadcasted_iota(jnp.int32, s.shape, 0) & (tq - 1)
            col = lax.broadcasted_iota(jnp.int32, s.shape, 1)
            s = jnp.where(col <= row, s, NEG_INF)
        cols = [s[:, c * LANES:(c + 1) * LANES] for c in range(nc)]
        pm = functools.reduce(jnp.maximum, cols)
        m_old = m_scr[...]
        m_new = jnp.maximum(m_old, jnp.max(pm, -1, keepdims=True))
        corr = jnp.exp2(m_old - m_new)
        ps = [jnp.exp2(c - m_new) for c in cols]
        l_scr[...] = corr * l_scr[...] + functools.reduce(jnp.add, ps)
        p = jnp.concatenate([x.astype(BF16) for x in ps], axis=1)
        acc_scr[...] = corr * acc_scr[...] + jnp.dot(p, vb, preferred_element_type=F32)
        m_scr[...] = m_new

    def body(j, carry):
        step(j, False)
        return carry

    lax.fori_loop(0, i, body, 0)
    step(i, True)

    lp = lam_ref[...]
    lam = (jnp.exp(jnp.sum(lp[0:1] * lp[1:2], -1, keepdims=True))
           - jnp.exp(jnp.sum(lp[2:3] * lp[3:4], -1, keepdims=True)) + lam_init)
    o2 = acc_scr[...] / jnp.sum(l_scr[...], -1, keepdims=True)
    o = o2[:tq] - lam * o2[tq:]
    o = o * lax.rsqrt(jnp.mean(o * o, -1, keepdims=True) + 1e-5) * g_ref[...]
    o_ref[0] = (o * (1.0 - lam_init)).astype(o_ref.dtype)


def _diff_attention(qkv, lam_p, subln_g, heads, lam_init):
    b, t, _ = qkv.shape
    tq = min(512, t)
    assert tq & (tq - 1) == 0 and t % tq == 0
    return pl.pallas_call(
        functools.partial(_diff_kernel, tq=tq, lam_init=lam_init),
        grid=(b, heads, t // tq),
        in_specs=[pl.BlockSpec((1, tq, LANES), lambda bi, h, i: (bi, i, h)),
                  pl.BlockSpec((1, t, LANES), lambda bi, h, i: (bi, 0, heads + h)),
                  pl.BlockSpec((1, t, LANES), lambda bi, h, i: (bi, 0, 2 * heads + h)),
                  pl.BlockSpec(lam_p.shape, lambda bi, h, i: (0, 0)),
                  pl.BlockSpec((1, LANES), lambda bi, h, i: (0, 0))],
        out_specs=pl.BlockSpec((1, tq, LANES), lambda bi, h, i: (bi, i, h)),
        out_shape=jax.ShapeDtypeStruct((b, t, heads * LANES), BF16),
        scratch_shapes=[pltpu.VMEM((2 * tq, LANES), BF16), pltpu.VMEM((2 * tq, LANES), F32),
                        pltpu.VMEM((2 * tq, LANES), F32), pltpu.VMEM((2 * tq, LANES), F32)],
        compiler_params=_params("parallel", "parallel", "arbitrary"),
        name="diff_attn",
    )(qkv, qkv, qkv, lam_p, subln_g)


def _gdn_prep_kernel(alog_ref, dtb_ref, q_ref, k_ref, v_ref, qh_ref, kh_ref, vh_ref, wq_ref, wk_ref, wv_ref,
                     ba_ref, qo_ref, ko_ref, vo_ref, go_ref, bo_ref, buf, *, tm, heads):
    i = pl.program_id(1)
    h = pl.program_id(2)

    def conv(cur_ref, halo_ref, w_ref):
        buf[0:SUBLANES, :] = jnp.where(i > 0, halo_ref[0], 0.0)
        buf[SUBLANES:, :] = cur_ref[0]
        acc = None
        for kk in range(GDN_CONV):
            term = buf[pl.ds(SUBLANES - (GDN_CONV - 1) + kk, tm), :] * w_ref[kk:kk + 1, :]
            acc = term if acc is None else acc + term
        return _silu(acc)

    def l2n(x):
        return x * lax.rsqrt(jnp.sum(x * x, -1, keepdims=True) + 1e-6)

    qo_ref[0] = l2n(conv(q_ref, qh_ref, wq_ref)) * (GDN_HEAD_DIM ** -0.5)
    ko_ref[0] = l2n(conv(k_ref, kh_ref, wk_ref))
    vo_ref[0] = conv(v_ref, vh_ref, wv_ref)
    ba = ba_ref[0]
    lane = lax.broadcasted_iota(jnp.int32, ba.shape, 1)
    b_in = jnp.sum(jnp.where(lane == h, ba, 0.0), -1, keepdims=True)
    a_in = jnp.sum(jnp.where(lane == h + heads, ba, 0.0), -1, keepdims=True)
    beta = jax.nn.sigmoid(b_in)
    a_log = jnp.full((1, LANES), alog_ref[h], F32)
    dt_bias = jnp.full((1, LANES), dtb_ref[h], F32)
    bo_ref[0] = jnp.broadcast_to(beta, (tm, LANES))
    go_ref[0] = -jnp.exp(a_log) * _softplus(a_in + dt_bias)


def _gdn_prep(proj, ba, conv_w, a_log, dt_bias, heads):
    b, t, _ = proj.shape
    tm = min(512, t)
    hb = tm // SUBLANES
    cur = lambda off: pl.BlockSpec((1, tm, LANES), lambda bi, i, h: (bi, i, off + h))
    halo = lambda off: pl.BlockSpec((1, SUBLANES, LANES),
                                    lambda bi, i, h: (bi, jnp.maximum(i * hb - 1, 0), off + h))
    cw = lambda off: pl.BlockSpec((GDN_CONV, LANES), lambda bi, i, h: (0, off + h))
    out = jax.ShapeDtypeStruct((b, t, heads * LANES), F32)
    ospec = pl.BlockSpec((1, tm, LANES), lambda bi, i, h: (bi, i, h))
    return pl.pallas_call(
        functools.partial(_gdn_prep_kernel, tm=tm, heads=heads),
        grid=(b, t // tm, heads),
        in_specs=[pl.BlockSpec(memory_space=pltpu.SMEM), pl.BlockSpec(memory_space=pltpu.SMEM),
                  cur(0), cur(heads), cur(2 * heads), halo(0), halo(heads), halo(2 * heads),
                  cw(0), cw(heads), cw(2 * heads),
                  pl.BlockSpec((1, tm, LANES), lambda bi, i, h: (bi, i, 0))],
        out_specs=[ospec] * 5,
        out_shape=[out] * 5,
        scratch_shapes=[pltpu.VMEM((tm + SUBLANES, LANES), F32)],
        compiler_params=_params("parallel", "parallel", "arbitrary"),
        name="gdn_prep",
    )(a_log, dt_bias, proj, proj, proj, proj, proj, proj, conv_w, conv_w, conv_w, ba)


def _gdn_scan_kernel(q_ref, k_ref, v_ref, g_ref, beta_ref, z_ref, ng_ref, o_ref, s_scr, *, c, cb):
    @pl.when(pl.program_id(2) == 0)
    def _():
        s_scr[...] = jnp.zeros_like(s_scr)

    r2 = 2 * c
    nd = cb // r2
    ri = lax.broadcasted_iota(jnp.int32, (r2, r2), 0)
    ci = lax.broadcasted_iota(jnp.int32, (r2, r2), 1)
    same = (ri < c) == (ci < c)
    causal = same & (ri >= ci)
    strict = same & (ri > ci)
    first = ri < c
    tril = jnp.where(causal, 1.0, 0.0).astype(BF16)
    ones = jnp.ones((r2, LANES), BF16)
    tiles = range(nd)
    sl = [pl.ds(d * r2, r2) for d in tiles]
    q = [q_ref[0, s, :] for s in sl]
    k = [k_ref[0, s, :] for s in sl]
    v = [v_ref[0, s, :] for s in sl]
    beta = [beta_ref[0, s, :] for s in sl]
    gc = [_dot_exact01(tril, g_ref[0, s, :]) for s in sl]
    gc_row = [_dot_exact01(ones, jnp.where(ci == 0, x, 0.0), _NT) for x in gc]
    decay = [jnp.where(causal, jnp.exp(jnp.minimum(x - y, 0.0)), 0.0) for x, y in zip(gc, gc_row)]
    kb = [x * y for x, y in zip(k, beta)]
    vb = [x * y for x, y in zip(v, beta)]
    lmat = [jnp.where(strict, _dot1(x, y, _NT) * z, 0.0) for x, y, z in zip(kb, k, decay)]
    a_in = [jnp.where(causal, _dot1(x, y, _NT) * z, 0.0) for x, y, z in zip(q, k, decay)]
    tinv = _inv_unit_lower_many([-x for x in lmat], r2, c)
    eg = [jnp.exp(x) for x in gc]
    u_c = [_dot1(x, y) for x, y in zip(tinv, vb)]
    w_c = [_dot1(x, y * z) for x, y, z in zip(tinv, kb, eg)]
    g_last = [(x[c - 1:c, :], x[r2 - 1:r2, :]) for x in gc]
    k_dec = [x * jnp.exp(jnp.where(first, ga, gb) - y) for x, (ga, gb), y in zip(k, g_last, gc)]
    q_dec = [x * y for x, y in zip(q, eg)]
    halves = [(d, h) for d in tiles for h in (slice(0, c), slice(c, r2))]
    gm = [_dot1(k_dec[d][h], w_c[d][h], _TN) for d, h in halves]
    cm = [_dot1(k_dec[d][h], u_c[d][h], _TN) for d, h in halves]
    gl = [jnp.exp(g_last[d][0 if h.start == 0 else 1]) for d, h in halves]
    states = [s_scr[...]]
    for x, y, z in zip(gm, cm, gl):
        st = states[-1]
        states.append(st * z - _dot1(x, st) + y)
    s_scr[...] = states[-1]
    for d in tiles:
        st_a, st_b = states[2 * d], states[2 * d + 1]
        vn = jnp.concatenate([u_c[d][:c] - _dot1(w_c[d][:c], st_a), u_c[d][c:] - _dot1(w_c[d][c:], st_b)], axis=0)
        oq = jnp.concatenate([_dot1(q_dec[d][:c], st_a), _dot1(q_dec[d][c:], st_b)], axis=0)
        o = oq + _dot1(a_in[d], vn)
        o = o * lax.rsqrt(jnp.mean(o * o, -1, keepdims=True) + 1e-6) * ng_ref[...]
        o_ref[0, sl[d], :] = (o * _silu(z_ref[0, sl[d], :])).astype(o_ref.dtype)


def _gdn_scan(q, k, v, g, beta, proj, norm_g, heads):
    b, t, _ = q.shape
    cb = min(512, t)
    blk = pl.BlockSpec((1, cb, LANES), lambda bi, h, ci: (bi, ci, h))
    return pl.pallas_call(
        functools.partial(_gdn_scan_kernel, c=CHUNK, cb=cb),
        grid=(b, heads, t // cb),
        in_specs=[blk, blk, blk, blk, blk,
                  pl.BlockSpec((1, cb, LANES), lambda bi, h, ci: (bi, ci, 3 * heads + h)),
                  pl.BlockSpec((1, LANES), lambda bi, h, ci: (0, 0))],
        out_specs=blk,
        out_shape=jax.ShapeDtypeStruct((b, t, heads * LANES), BF16),
        scratch_shapes=[pltpu.VMEM((LANES, LANES), F32)],
        compiler_params=_params("parallel", "parallel", "arbitrary"),
        name="gdn_scan",
    )(q, k, v, g, beta, proj, norm_g)


def _rwkv_proj_kernel(x_ref, halo_ref, sc_ref, sh_ref, mu_ref, wrkv_ref, w0_ref, w1_ref, w2_ref, a0_ref, a1_ref,
                      a2_ref, g1_ref, g2_ref, r_ref, k_ref, v_ref, lw_ref, a_ref, g_ref, buf, *, tm):
    i = pl.program_id(1)
    sc1 = 1.0 + sc_ref[0]
    sh = sh_ref[0]
    u = x_ref[0] * sc1 + sh
    buf[0:SUBLANES, :] = jnp.where(i > 0, halo_ref[0] * sc1 + sh, 0.0)
    buf[SUBLANES:, :] = u
    xx = buf[pl.ds(SUBLANES - 1, tm), :] - u
    lerp = lambda n: (u + xx * mu_ref[n:n + 1, :]).astype(BF16)
    dot = lambda a, w: jnp.dot(a, w, preferred_element_type=F32)
    r_ref[0] = dot(lerp(0), wrkv_ref[0])
    k_ref[0] = dot(lerp(1), wrkv_ref[1])
    v_ref[0] = dot(lerp(2), wrkv_ref[2])
    hw = jnp.tanh(dot(lerp(3), w1_ref[...])).astype(BF16)
    w_log = -_softplus(-(w0_ref[...] + dot(hw, w2_ref[...]))) - 0.5
    lw_ref[0] = -jnp.exp(w_log)
    ha = dot(lerp(4), a1_ref[...]).astype(BF16)
    a_ref[0] = jax.nn.sigmoid(a0_ref[...] + dot(ha, a2_ref[...]))
    hg = jax.nn.sigmoid(dot(lerp(5), g1_ref[...])).astype(BF16)
    g_ref[0] = dot(hg, g2_ref[...])


def _rwkv_proj(x, sc, sh, mu, w_rkv, w0, w1, w2, a0, a1, a2, g1, g2):
    b, t, d = x.shape
    tm = min(256, t)
    hb = tm // SUBLANES
    full = lambda a: pl.BlockSpec(a.shape, lambda bi, i: (0,) * a.ndim)
    row = pl.BlockSpec((1, tm, d), lambda bi, i: (bi, i, 0))
    mod = pl.BlockSpec((1, 1, d), lambda bi, i: (bi, 0, 0))
    out = jax.ShapeDtypeStruct((b, t, d), F32)
    consts = [mu, w_rkv, w0, w1, w2, a0, a1, a2, g1, g2]
    return pl.pallas_call(
        functools.partial(_rwkv_proj_kernel, tm=tm),
        grid=(b, t // tm),
        in_specs=[row, pl.BlockSpec((1, SUBLANES, d), lambda bi, i: (bi, jnp.maximum(i * hb - 1, 0), 0)),
                  mod, mod] + [full(a) for a in consts],
        out_specs=[row] * 6,
        out_shape=[out] * 6,
        scratch_shapes=[pltpu.VMEM((tm + SUBLANES, d), F32)],
        compiler_params=_params("parallel", "arbitrary"),
        name="rwkv_proj",
    )(x, x, sc, sh, *consts)


def _rwkv_scan_kernel(r_ref, k_ref, v_ref, lw_ref, a_ref, g_ref, kk_ref, ka_ref, rk_ref, gg_ref, gb_ref,
                      o_ref, h_scr, *, c, cb):
    n = RWKV_HEAD_DIM

    @pl.when(pl.program_id(2) == 0)
    def _():
        h_scr[...] = jnp.zeros_like(h_scr)

    r2 = 2 * c
    ri = lax.broadcasted_iota(jnp.int32, (r2, r2), 0)
    ci = lax.broadcasted_iota(jnp.int32, (r2, r2), 1)
    same = (ri < c) == (ci < n)
    ti, tj = ri & (c - 1), ci & (c - 1)
    incl = same & (ti >= tj)
    strict = same & (ti > tj)
    head0 = lax.broadcasted_iota(jnp.int32, (c, LANES), 1) < n
    stack = lambda x: jnp.concatenate([jnp.where(head0, x, 0.0), jnp.where(head0, 0.0, x)], axis=0)
    unstack = lambda x2: x2[:c] + x2[c:]
    bri = lax.broadcasted_iota(jnp.int32, (cb, cb), 0)
    bci = lax.broadcasted_iota(jnp.int32, (cb, cb), 1)
    tril = jnp.where(((bri ^ bci) < c) & (bri >= bci), 1.0, 0.0).astype(BF16)
    lw_all = lw_ref[0]
    lc_all = _dot_exact01(tril, lw_all)

    chunks = range(cb // c)
    sl = [pl.ds(s * c, c) for s in chunks]
    lo = [s * c for s in chunks]
    r = [r_ref[0, s, :] for s in sl]
    k = [k_ref[0, s, :] for s in sl]
    v2 = [stack(v_ref[0, s, :]) for s in sl]
    a = [a_ref[0, s, :] for s in sl]
    lw = [lw_all[o:o + c] for o in lo]
    lc = [lc_all[o:o + c] for o in lo]
    kkr2 = [stack(x * kk_ref[...]) for x in k]
    kk2 = [x * lax.rsqrt(jnp.sum(x * x, -1, keepdims=True) + 1e-6) for x in kkr2]
    k2 = [x * (1.0 + (y - 1.0) * ka_ref[...]) for x, y in zip(k, a)]
    lend = [x[c - 1:c, :] for x in lc]
    a2 = [-x * stack(jnp.exp(y - z)) for x, y, z in zip(kk2, lc, lw)]
    r2s = [stack(x * jnp.exp(y)) for x, y in zip(r, lc)]
    b2v = [x * stack(y) for x, y in zip(kk2, a)]
    einv = [stack(jnp.exp(-x)) for x in lc]
    eend = [stack(jnp.exp(e - x)) for e, x in zip(lend, lc)]
    k2s = [stack(x) for x in k2]
    bt = [x * y for x, y in zip(b2v, einv)]
    kt = [x * y for x, y in zip(k2s, einv)]
    bh = [x * y for x, y in zip(b2v, eend)]
    kh = [x * y for x, y in zip(k2s, eend)]
    n_ab = [jnp.where(strict, _dot1(x, y, _NT), 0.0) for x, y in zip(a2, bt)]
    a_ak = [jnp.where(strict, _dot1(x, y, _NT), 0.0) for x, y in zip(a2, kt)]
    a_rb = [jnp.where(incl, _dot1(x, y, _NT), 0.0) for x, y in zip(r2s, bt)]
    a_rk = [jnp.where(incl, _dot1(x, y, _NT), 0.0) for x, y in zip(r2s, kt)]
    tinv = _inv_unit_lower_many(n_ab, r2, c)
    w2 = [_dot1(x, y) for x, y in zip(tinv, a2)]
    akv = [_dot1(x, y) for x, y in zip(a_ak, v2)]
    u0 = [_dot1(x, y) for x, y in zip(tinv, akv)]
    y0 = [_dot1(x, y) for x, y in zip(a_rk, v2)]
    kv = [_dot1(x, y, _TN) for x, y in zip(v2, kh)]
    bonus2 = [jnp.sum(stack(x * y * rk_ref[...]), -1, keepdims=True) * z for x, y, z in zip(r, k2, v2)]
    gm = [_dot1(x, y, _TN) for x, y in zip(w2, bh)]
    cm = [_dot1(x, y, _TN) + z for x, y, z in zip(u0, bh, kv)]
    states = [h_scr[...]]
    for s in chunks:
        ht = states[-1]
        states.append(ht * jnp.exp(lend[s]) + _dot1(ht, gm[s]) + cm[s])
    h_scr[...] = states[-1]
    for s in chunks:
        ht = states[s]
        u2 = _dot1(w2[s], ht, _NT) + u0[s]
        y2 = _dot1(r2s[s], ht, _NT) + _dot1(a_rb[s], u2) + y0[s]
        mean = jnp.sum(y2, -1, keepdims=True) * (1.0 / n)
        yc = jnp.where(same, y2 - mean, 0.0)
        var = jnp.sum(yc * yc, -1, keepdims=True) * (1.0 / n)
        yn = unstack(yc * lax.rsqrt(var + RWKV_GN_EPS)) * gg_ref[...] + gb_ref[...]
        o_ref[0, sl[s], :] = ((yn + unstack(bonus2[s])) * g_ref[0, sl[s], :]).astype(o_ref.dtype)


def _rwkv_scan(r, k, v, lw, a, g, k_k, k_a, r_k, gn_g, gn_b):
    b, t, d = r.shape
    cb = min(512, t)
    blk = pl.BlockSpec((1, cb, LANES), lambda bi, p, ci: (bi, ci, p))
    par = pl.BlockSpec((1, LANES), lambda bi, p, ci: (0, p))
    return pl.pallas_call(
        functools.partial(_rwkv_scan_kernel, c=CHUNK, cb=cb),
        grid=(b, d // LANES, t // cb),
        in_specs=[blk] * 6 + [par] * 5,
        out_specs=blk,
        out_shape=jax.ShapeDtypeStruct((b, t, d), BF16),
        scratch_shapes=[pltpu.VMEM((LANES, LANES), F32)],
        compiler_params=_params("parallel", "parallel", "arbitrary"),
        name="rwkv_scan",
    )(r, k, v, lw, a, g, k_k, k_a, r_k, gn_g, gn_b)


def _row(v):
    return v.reshape(1, -1)


def _rwkv7_mixer(x, sc, sh, mu, w_rkv, w0, w1, w2, a0, a1, a2, g1, g2, k_k, k_a, r_k, gn_g, gn_b):
    r, k, v, lw, a, g = _rwkv_proj(x, sc, sh, mu, w_rkv.astype(BF16), _row(w0), w1.astype(BF16),
                                   w2.astype(BF16), _row(a0), a1.astype(BF16), a2.astype(BF16),
                                   g1.astype(BF16), g2.astype(BF16))
    return _rwkv_scan(r, k, v, lw, a, g, _row(k_k), _row(k_a), _row(r_k), _row(gn_g), _row(gn_b))


def _gdn_mixer(x, sc, sh, w_in, conv_w, a_log, dt_bias, norm_g):
    d = x.shape[-1]
    heads = a_log.shape[0]
    hd = heads * GDN_HEAD_DIM
    w_main = w_in[:, :4 * hd].astype(BF16)
    w_ba = jnp.pad(w_in[:, 4 * hd:], ((0, 0), (0, LANES - 2 * heads))).astype(BF16)
    zeros = lambda m: jnp.zeros((1, m), F32)
    ones = lambda m: jnp.ones((1, m), F32)
    proj = _proj(x, sc, sh, w_main, zeros(4 * hd), ones(4 * hd), tn=512, out_dtype=F32)
    ba = _proj(x, sc, sh, w_ba, zeros(LANES), ones(LANES), tn=LANES, out_dtype=F32)
    q, k, v, g, beta = _gdn_prep(proj, ba, conv_w, a_log, dt_bias, heads)
    return _gdn_scan(q, k, v, g, beta, proj, _row(norm_g), heads)


def _diff_mixer(x, sc, sh, w_in, lam_p, subln_g, lam_init, rope):
    d = x.shape[-1]
    heads = d // (2 * DIFF_HEAD_DIM)
    colscale = jnp.concatenate([jnp.full((1, d), DIFF_HEAD_DIM ** -0.5 * LOG2E, F32), jnp.ones((1, 2 * d), F32)], -1)
    qkv = _proj(x, sc, sh, w_in.astype(BF16), jnp.zeros((1, 3 * d), F32), colscale, tn=512, out_dtype=BF16,
                rope=rope(512), rope_cols=2 * d)
    return _diff_attention(qkv, lam_p, _row(subln_g), heads, lam_init)


def _swa_mixer(x, sc, sh, w_qkv, b_qkv, sinks, rope):
    hq = sinks.shape[0]
    qd = hq * SWA_HEAD_DIM
    m = w_qkv.shape[1]
    colscale = jnp.concatenate([jnp.full((1, qd), SWA_HEAD_DIM ** -0.5, F32), jnp.ones((1, m - qd), F32)], -1)
    qkv = _proj(x, sc, sh, w_qkv.astype(BF16), _row(b_qkv), colscale, tn=LANES, out_dtype=BF16,
                rope=rope(LANES), rope_cols=qd + SWA_KV_HEADS * SWA_HEAD_DIM)
    return _swa_attention(qkv, sinks, hq)


def kernel(x, c, ada_w, ada_b, ln_g, ln_b, ffn_w_in, ffn_w_out, rwkv_mu, rwkv_w_rkv, rwkv_w0, rwkv_w1, rwkv_w2, rwkv_a0, rwkv_a1, rwkv_a2, rwkv_g1, rwkv_g2, rwkv_k_k, rwkv_k_a, rwkv_r_k, rwkv_gn_g, rwkv_gn_b, rwkv_w_out, gdn_w_in, gdn_conv, gdn_a_log, gdn_dt_bias, gdn_norm_g, gdn_w_out, diff_w_in, diff_lambda, diff_subln_g, diff_w_out, swa_w_qkv, swa_b_qkv, swa_sinks, swa_w_out, swa_b_out):
    b, t, d = x.shape
    depth = ada_w.shape[0]
    alpha = (2.0 * depth) ** 0.25
    mod = _adaln_mod(c, ada_w, ada_b)
    rope = functools.lru_cache(None)(lambda width: _rope_tables(t, width))
    zero_bias = jnp.zeros((1, d), F32)
    for i in range(depth):
        sh1, sc1, ga1, sh2, sc2, ga2 = (mod[i, :, None, n * d:(n + 1) * d] for n in range(6))
        m, j = i % N_MIXERS, i // N_MIXERS
        if m == 0:
            y = _rwkv7_mixer(x, sc1, sh1, rwkv_mu[j], rwkv_w_rkv[j], rwkv_w0[j], rwkv_w1[j], rwkv_w2[j],
                             rwkv_a0[j], rwkv_a1[j], rwkv_a2[j], rwkv_g1[j], rwkv_g2[j], rwkv_k_k[j],
                             rwkv_k_a[j], rwkv_r_k[j], rwkv_gn_g[j], rwkv_gn_b[j])
            w_out, b_out = rwkv_w_out[j], zero_bias
        elif m == 1:
            y = _gdn_mixer(x, sc1, sh1, gdn_w_in[j], gdn_conv[j], gdn_a_log[j], gdn_dt_bias[j], gdn_norm_g[j])
            w_out, b_out = gdn_w_out[j], zero_bias
        elif m == 2:
            lam_init = 0.8 - 0.6 * math.exp(-0.3 * i)
            y = _diff_mixer(x, sc1, sh1, diff_w_in[j], diff_lambda[j], diff_subln_g[j], lam_init, rope)
            w_out, b_out = diff_w_out[j], zero_bias
        else:
            y = _swa_mixer(x, sc1, sh1, swa_w_qkv[j], swa_b_qkv[j], swa_sinks[j], rope)
            w_out, b_out = swa_w_out[j], _row(swa_b_out[j])
        x = _outproj_ln(y, w_out.astype(BF16), b_out, x, ga1, _row(ln_g[i, 0]), _row(ln_b[i, 0]), alpha)
        x = _ffn(x, sc2, sh2, ga2, ffn_w_in[i].astype(BF16), ffn_w_out[i].astype(BF16),
                 _row(ln_g[i, 1]), _row(ln_b[i, 1]), alpha)
    return x
```
